```python
import jax
import jax.numpy as jnp
from jax import lax
import numpy as np

D_MODEL = 1024
BATCH = 32
SEQ = 256
DEPTH = 2
DEC_BATCH = 8
DEC_SEQ = 1024
PAST_LEN = 512

GRID_W = 64
BR_WIDTH = 512
N_BRANCH = 3
RET_HEADS = 4
RET_DK = BR_WIDTH // RET_HEADS
RET_DV = BR_WIDTH // RET_HEADS
RET_CHUNK = 128
NA_HEADS = 4
NA_HD = BR_WIDTH // NA_HEADS
NA_ROWS = 8
NA_COLS = 16
NA_QBLOCK = 16
NA_KBLOCK = 32
CTX_QBLOCK = 128
HG_HEADS = 4
HG_DK = BR_WIDTH // HG_HEADS
HG_DV = BR_WIDTH // HG_HEADS
HG_CHUNK = 32
ROPE_BASE = 10000.0
D_FF = 2816
N_EXPERTS = 8
TOP_K = 2
N_DENSE = (DEPTH + 1) // 2
N_MOE = DEPTH // 2
N_MOD = 6
EPS = 1e-6
F_MIN = 1e-30
NEG_INF = -1e30
D_IN = 12 * BR_WIDTH + N_BRANCH * D_MODEL
SPLITS = [BR_WIDTH * i for i in range(1, 13)]
F32 = jnp.float32

kernel_name = 'hybrid_prefix_diffusion_trunk_step'


def rmsnorm(x, g):
    xf = x.astype(F32)
    y = xf * lax.rsqrt(jnp.mean(xf * xf, axis=-1, keepdims=True) + EPS)
    return (y * g.astype(F32)).astype(x.dtype)


def head_layernorm(o):
    of = o.astype(F32)
    mu = jnp.mean(of, axis=-1, keepdims=True)
    var = jnp.mean(jnp.square(of - mu), axis=-1, keepdims=True)
    return ((of - mu) * lax.rsqrt(var + 1e-5)).astype(o.dtype)


def heads(x, n):
    return x.reshape(x.shape[:-1] + (n, x.shape[-1] // n))


def flip(a):
    return jnp.flip(a, axis=1)


def rope_2d(x):
    T, hd = x.shape[1], x.shape[-1]
    nf = hd // 4
    t = jnp.arange(T)
    pos = jnp.stack([t // GRID_W, t % GRID_W], axis=-1).astype(F32)
    inv = ROPE_BASE ** (-jnp.arange(nf, dtype=F32) / nf)
    ang = pos[:, :, None] * inv
    cos = jnp.cos(ang)[None, :, None]
    sin = jnp.sin(ang)[None, :, None]
    xr = x.astype(F32).reshape(x.shape[:-1] + (2, 2, nf))
    x1, x2 = xr[..., 0, :], xr[..., 1, :]
    y = jnp.stack([x1 * cos - x2 * sin, x2 * cos + x1 * sin], axis=-2)
    return y.reshape(x.shape).astype(x.dtype)


def to_chunks(x, c):
    B, T, H, d = x.shape
    return x.reshape(B, T // c, c, H, d).transpose(1, 0, 3, 2, 4)


def from_chunks(y):
    N, B, H, c, d = y.shape
    return y.transpose(1, 0, 3, 2, 4).reshape(B, N * c, H, d)


def retention_scan(q, k, v, log_gamma, s0):
    dtype = v.dtype
    qc, kc, vc = (to_chunks(a.astype(F32), RET_CHUNK) for a in (q, k, v))
    idx = jnp.arange(RET_CHUNK, dtype=F32)
    diff = idx[:, None] - idx[None, :]
    lg = log_gamma.astype(F32)
    dmat = jnp.where(diff >= 0, jnp.exp(lg[:, None, None] * jnp.maximum(diff, 0.0)), 0.0)
    q_dec = jnp.exp(lg[:, None] * (idx + 1.0))[..., None]
    k_dec = jnp.exp(lg[:, None] * (RET_CHUNK - 1.0 - idx))[..., None]
    c_dec = jnp.exp(lg * RET_CHUNK)[:, None, None]

    def step(s, inp):
        qi, ki, vi = inp
        att = jnp.einsum('bhnd,bhmd->bhnm', qi, ki) * dmat
        o = jnp.einsum('bhnm,bhme->bhne', att, vi) + jnp.einsum('bhnd,bhde->bhne', qi * q_dec, s)
        s = s * c_dec + jnp.einsum('bhmd,bhme->bhde', ki * k_dec, vi)
        return s, o

    s, o = lax.scan(step, s0.astype(F32), (qc, kc, vc))
    return from_chunks(o).astype(dtype), s


def hgrn2_scan(q, k, v, logf, s0):
    dtype = v.dtype
    qc, kc, vc, gc = (to_chunks(a.astype(F32), HG_CHUNK) for a in (q, k, v, logf))
    causal = jnp.tril(jnp.ones((HG_CHUNK, HG_CHUNK), dtype=bool))[:, :, None]
    causal_f = causal.astype(F32)

    def step(s, inp):
        qi, ki, vi, gi = inp
        a = jnp.cumsum(gi, axis=2)
        diff = a[:, :, :, None, :] - a[:, :, None, :, :]
        pair = jnp.exp(jnp.where(causal, diff, 0.0)) * causal_f
        att = jnp.einsum('bhnd,bhmd,bhnmd->bhnm', qi, ki, pair)
        o = jnp.einsum('bhnm,bhme->bhne', att, vi) + jnp.einsum('bhnd,bhde->bhne', qi * jnp.exp(a), s)
        a_last = a[:, :, -1:, :]
        s = s * jnp.exp(a_last[:, :, 0, :, None]) + jnp.einsum('bhmd,bhme->bhde', ki * jnp.exp(a_last - a), vi)
        return s, o

    s, o = lax.scan(step, s0.astype(F32), (qc, kc, vc, gc))
    return from_chunks(o).astype(dtype), s


def ctx_attention(q, k, v):
    B, L, H, hd = q.shape
    nb = L // CTX_QBLOCK
    qb = q.reshape(B, nb, CTX_QBLOCK, H, hd).transpose(1, 0, 2, 3, 4)

    def blk(qi):
        s = jnp.einsum('bqhd,bkhd->bhqk', qi, k).astype(F32) * (hd ** -0.5)
        p = jax.nn.softmax(s, axis=-1).astype(v.dtype)
        return jnp.einsum('bhqk,bkhd->bqhd', p, v)

    o = lax.map(blk, qb)
    return o.transpose(1, 0, 2, 3, 4).reshape(B, L, H, hd)


def na_latent(q, k, v, k_ctx, v_ctx, rpb):
    B, T, H, hd = q.shape
    rows = T // GRID_W
    kr = min(NA_ROWS, rows)
    ncb = GRID_W // NA_QBLOCK
    scale = hd ** -0.5
    qcol = np.arange(GRID_W)
    cs = np.clip(qcol - NA_COLS // 2, 0, GRID_W - NA_COLS).reshape(ncb, NA_QBLOCK)
    kstart = np.clip(np.arange(ncb) * NA_QBLOCK - NA_COLS // 2, 0, GRID_W - NA_KBLOCK)
    kcol = kstart[:, None] + np.arange(NA_KBLOCK)
    qcol_b = qcol.reshape(ncb, NA_QBLOCK)
    col_ok = (kcol[:, None, :] >= cs[:, :, None]) & (kcol[:, None, :] < cs[:, :, None] + NA_COLS)
    col_off = np.clip(kcol[:, None, :] - qcol_b[:, :, None] + NA_COLS - 1, 0, 2 * NA_COLS - 2)
    kgc = k.reshape(B, rows, GRID_W, H, hd)[:, :, kcol]
    vgc = v.reshape(B, rows, GRID_W, H, hd)[:, :, kcol]
    q_rows = q.reshape(B, rows, ncb, NA_QBLOCK, H, hd).transpose(1, 0, 2, 3, 4, 5)
    nloc = kr * NA_KBLOCK

    def row_fn(args):
        r, q_r = args
        rs = jnp.clip(r - kr // 2, 0, rows - kr)
        k_r = lax.dynamic_slice_in_dim(kgc, rs, kr, axis=1)
        v_r = lax.dynamic_slice_in_dim(vgc, rs, kr, axis=1)
        row_off = rs + jnp.arange(kr) - r + (NA_ROWS - 1)
        bias = rpb[:, row_off][:, :, col_off].transpose(0, 2, 3, 1, 4)
        s_loc = jnp.einsum('bjqhd,bijchd->bhjqic', q_r, k_r).astype(F32) * scale + bias[None].astype(F32)
        s_loc = jnp.where(col_ok[None, None, :, :, None, :], s_loc, NEG_INF)
        s_ctx = jnp.einsum('bjqhd,bphd->bhjqp', q_r, k_ctx).astype(F32) * scale
        logits = jnp.concatenate([s_loc.reshape(s_loc.shape[:4] + (nloc,)), s_ctx], axis=-1)
        prob = jax.nn.softmax(logits, axis=-1).astype(v_r.dtype)
        p_loc = prob[..., :nloc].reshape(s_loc.shape)
        o = (jnp.einsum('bhjqic,bijchd->bjqhd', p_loc, v_r)
             + jnp.einsum('bhjqp,bphd->bjqhd', prob[..., nloc:], v_ctx))
        return o.reshape(B, GRID_W, H, hd)

    o = lax.map(row_fn, (jnp.arange(rows), q_rows))
    return o.transpose(1, 0, 2, 3, 4).reshape(B, T, H, hd)


def hg_logf(z, lb):
    f = lb + (1.0 - lb) * jax.nn.sigmoid(z.astype(F32))
    return jnp.log(jnp.maximum(f, F_MIN))


def token_mixers(h, l, P, hg_lower, cache):
    B, T, _ = h.shape
    u = h @ P['w_in'][l]
    rq, rk, rv, rg, nq, nk, nv, hq, hf_f, hf_b, hi, hgate, mg = jnp.split(u, SPLITS, axis=-1)
    latent = cache is not None
    if latent:
        k_ctx, v_ctx, s_ret0, s_hg0 = cache
    else:
        s_ret0 = jnp.zeros((B, 2, RET_HEADS, RET_DK, RET_DV), F32)
        s_hg0 = jnp.zeros((B, 2, HG_HEADS, HG_DK, HG_DV), F32)

    q = heads(rq, RET_HEADS)
    k = heads(rk, RET_HEADS) * (RET_DK ** -0.5)
    v = heads(rv, RET_HEADS)
    if latent:
        q, k = rope_2d(q), rope_2d(k)
    lg = jax.nn.log_sigmoid(P['ret_decay'][l].astype(F32))
    o_f, sr_f = retention_scan(q, k, v, lg[0], s_ret0[:, 0])
    o_b, sr_b = retention_scan(flip(q), flip(k), flip(v), lg[1], s_ret0[:, 1])
    ret_o = head_layernorm(o_f + flip(o_b)).reshape(B, T, BR_WIDTH) * jax.nn.silu(rg)

    na_q, na_k, na_v = heads(nq, NA_HEADS), heads(nk, NA_HEADS), heads(nv, NA_HEADS)
    if latent:
        na = na_latent(na_q, na_k, na_v, k_ctx, v_ctx, P['na_rpb'][l])
    else:
        na = ctx_attention(na_q, na_k, na_v)
    na_o = na.reshape(B, T, BR_WIDTH)

    lb = hg_lower[l]
    lf_f = heads(hg_logf(hf_f, lb[0]), HG_HEADS)
    lf_b = flip(heads(hg_logf(hf_b, lb[1]), HG_HEADS))
    gq = heads(jax.nn.silu(hq), HG_HEADS)
    gi = heads(hi, HG_HEADS)
    o_f, sh_f = hgrn2_scan(gq, -jnp.expm1(lf_f), gi, lf_f, s_hg0[:, 0])
    o_b, sh_b = hgrn2_scan(flip(gq), -jnp.expm1(lf_b), flip(gi), lf_b, s_hg0[:, 1])
    hg_o = rmsnorm(o_f + flip(o_b), P['hg_norm'][l].reshape(HG_HEADS, HG_DV)).reshape(B, T, BR_WIDTH) * jax.nn.silu(hgate)

    br = jnp.stack([ret_o, na_o, hg_o], axis=0)
    proj = jnp.einsum('nbtw,nwd->btnd', br, P['w_br'][l])
    gates = jax.nn.sigmoid(mg.reshape(B, T, N_BRANCH, D_MODEL))
    out = jnp.sum(gates * proj, axis=2) @ P['w_o'][l]
    if latent:
        return out, None
    return out, (na_k, na_v, jnp.stack([sr_f, sr_b], axis=1), jnp.stack([sh_f, sh_b], axis=1))


def swiglu(h, w_gu, w_down):
    a, b = jnp.split(h @ w_gu, 2, axis=-1)
    return (jax.nn.silu(a) * b) @ w_down


def moe(h, w_router, w_gu, w_down):
    logits = (h @ w_router).astype(F32)
    top_v, top_i = lax.top_k(logits, TOP_K)
    top_w = jax.nn.softmax(top_v, axis=-1)
    gate = jnp.sum(jax.nn.one_hot(top_i, N_EXPERTS, dtype=F32) * top_w[..., None], axis=-2).astype(h.dtype)
    out = jnp.zeros_like(h)
    for e in range(N_EXPERTS):
        out = out + gate[..., e:e + 1] * swiglu(h, w_gu[e], w_down[e])
    return out


def trunk_layer(x, cvec, l, P, hg_lower, cache):
    mod = jax.nn.silu(cvec) @ P['w_mod'][l] + P['b_mod'][l]
    mod = mod.reshape(cvec.shape[0], 1, N_MOD, D_MODEL)
    g = P['norm_g'][l]
    h = rmsnorm(x, g[0]) * (1.0 + mod[:, :, 1]) + mod[:, :, 0]
    m, ctx = token_mixers(h, l, P, hg_lower, cache)
    x = x + mod[:, :, 2] * rmsnorm(m, g[1])
    h = rmsnorm(x, g[2]) * (1.0 + mod[:, :, 4]) + mod[:, :, 3]
    if l % 2 == 0:
        f = swiglu(h, P['w_ffn_gu'][l // 2], P['w_ffn_down'][l // 2])
    else:
        f = moe(h, P['w_router'][l // 2], P['w_exp_gu'][l // 2], P['w_exp_down'][l // 2])
    x = x + mod[:, :, 5] * rmsnorm(f, g[3])
    return x, ctx


def setup_inputs(seed: int = 0) -> dict:
    key = jax.random.key(seed)
    ks = jax.random.split(key, 24)
    D, W = D_MODEL, BR_WIDTH

    def n(k, shape, s):
        return s * jax.random.normal(k, shape, F32)

    hh = jnp.arange(RET_HEADS, dtype=F32)
    gamma_logit = jnp.log1p(-(2.0 ** (-5.0 - hh))) + (5.0 + hh) * jnp.log(2.0)
    return {
        'x_prompt': n(ks[0], (BATCH, SEQ, D), 1.0),
        'x_sample': n(ks[1], (DEC_BATCH, DEC_SEQ, D), 1.0),
        'c': n(ks[2], (DEC_BATCH, D), 1.0),
        'cache_na_k': n(ks[3], (DEC_BATCH, DEPTH, PAST_LEN, NA_HEADS, NA_HD), 1.0),
        'cache_na_v': n(ks[4], (DEC_BATCH, DEPTH, PAST_LEN, NA_HEADS, NA_HD), 1.0),
        'state_ret': n(ks[5], (DEC_BATCH, DEPTH, 2, RET_HEADS, RET_DK, RET_DV), 1.0),
        'state_hgrn': n(ks[6], (DEC_BATCH, DEPTH, 2, HG_HEADS, HG_DK, HG_DV), 0.5),
        'c_ctx': n(ks[7], (D,), 1.0),
        'w_mod': n(ks[8], (DEPTH, D, N_MOD * D), 0.5 * D ** -0.5),
        'b_mod': n(ks[9], (DEPTH, N_MOD * D), 0.02),
        'norm_g': 1.0 + n(ks[10], (DEPTH, 4, D), 0.02),
        'w_in': n(ks[11], (DEPTH, D, D_IN), D ** -0.5),
        'ret_decay': gamma_logit + n(ks[12], (DEPTH, 2, RET_HEADS), 0.1),
        'na_rpb': n(ks[13], (DEPTH, NA_HEADS, 2 * NA_ROWS - 1, 2 * NA_COLS - 1), 0.1),
        'hg_lb': n(ks[14], (DEPTH, 2, HG_HEADS * HG_DK), 0.1),
        'hg_norm': 1.0 + n(ks[15], (DEPTH, HG_HEADS * HG_DV), 0.02),
        'w_br': n(ks[16], (DEPTH, N_BRANCH, W, D), W ** -0.5),
        'w_o': n(ks[17], (DEPTH, D, D), D ** -0.5),
        'w_ffn_gu': n(ks[18], (N_DENSE, D, 2 * D_FF), D ** -0.5),
        'w_ffn_down': n(ks[19], (N_DENSE, D_FF, D), D_FF ** -0.5),
        'w_router': n(ks[20], (N_MOE, D, N_EXPERTS), D ** -0.5),
        'w_exp_gu': n(ks[21], (N_MOE, N_EXPERTS, D, 2 * D_FF), D ** -0.5),
        'w_exp_down': n(ks[22], (N_MOE, N_EXPERTS, D_FF, D), D_FF ** -0.5),
    }


def reference(x_prompt, x_sample, c, cache_na_k, cache_na_v, state_ret, state_hgrn, c_ctx, w_mod, b_mod, norm_g,
              w_in, ret_decay, na_rpb, hg_lb, hg_norm, w_br, w_o, w_ffn_gu, w_ffn_down, w_router, w_exp_gu, w_exp_down):
    P = dict(w_mod=w_mod, b_mod=b_mod, norm_g=norm_g, w_in=w_in, ret_decay=ret_decay, na_rpb=na_rpb,
             hg_norm=hg_norm, w_br=w_br, w_o=w_o, w_ffn_gu=w_ffn_gu, w_ffn_down=w_ffn_down,
             w_router=w_router, w_exp_gu=w_exp_gu, w_exp_down=w_exp_down)
    p_lb = jax.nn.softmax(hg_lb.astype(F32), axis=0)
    hg_lower = jnp.cumsum(p_lb, axis=0) - p_lb[:1]

    x = x_prompt
    ks_l, vs_l, sr_l, sh_l = [], [], [], []
    for l in range(DEPTH):
        x, (k_l, v_l, s_r, s_h) = trunk_layer(x, c_ctx[None], l, P, hg_lower, None)
        ks_l.append(k_l)
        vs_l.append(v_l)
        sr_l.append(s_r)
        sh_l.append(s_h)
    y_prompt = x
    new_cache_na_k = jnp.stack(ks_l, axis=1)
    new_cache_na_v = jnp.stack(vs_l, axis=1)
    new_state_ret = jnp.stack(sr_l, axis=1)
    new_state_hgrn = jnp.stack(sh_l, axis=1)

    x = x_sample
    for l in range(DEPTH):
        cache = (cache_na_k[:, l], cache_na_v[:, l], state_ret[:, l], state_hgrn[:, l])
        x, _ = trunk_layer(x, c, l, P, hg_lower, cache)
    y_sample = x
    return (y_prompt, y_sample, new_cache_na_k, new_cache_na_v, new_state_ret, new_state_hgrn)
```

```python
import functools

import numpy as np
import jax
import jax.numpy as jnp
from jax import lax
from jax.experimental import pallas as pl
from jax.experimental.pallas import tpu as pltpu

F32 = jnp.float32
BF16 = jnp.bfloat16
HIGHEST = lax.Precision.HIGHEST

D_MODEL = 1024
BR_WIDTH = 512
N_HEADS = 4
HEAD_DIM = 128
GRID_W = 64
NA_ROWS = 8
NA_COLS = 16
D_FF = 2816
N_EXPERTS = 8
N_MOD = 6
EPS = 1e-6
LN_EPS = 1e-5
F_MIN = 1e-30
NEG_INF = -1e30
ROPE_BASE = 10000.0
T_CTX = 256
HG_BLOCK = 16
RET_QTILE = 256
MOD_ROWS = 16
LANES = 128
VMEM_LIMIT = 56 * 1024 * 1024

NT_DIMS = (((1,), (1,)), ((), ()))
TN_DIMS = (((0,), (0,)), ((), ()))


def _params(n_axes, vmem=VMEM_LIMIT):
    return pltpu.CompilerParams(dimension_semantics=("arbitrary",) * n_axes, vmem_limit_bytes=vmem)


def _silu(x):
    return x * jax.nn.sigmoid(x)


def _rms(x, g):
    return x * lax.rsqrt(jnp.mean(x * x, axis=-1, keepdims=True) + EPS) * g


def _mod_kernel(cv_ref, w_ref, b_ref, o_ref):
    s = _silu(cv_ref[...])
    o_ref[0] = jnp.dot(s, w_ref[0], preferred_element_type=F32, precision=HIGHEST) + b_ref[0]


def _modulation(cvec, w_mod, b_mod):
    depth, d, n = w_mod.shape
    tn = 1024
    return pl.pallas_call(
        _mod_kernel,
        grid=(depth, n // tn),
        in_specs=[pl.BlockSpec((MOD_ROWS, d), lambda l, j: (0, 0)),
                  pl.BlockSpec((1, d, tn), lambda l, j: (l, 0, j)),
                  pl.BlockSpec((1, 1, tn), lambda l, j: (l, 0, j))],
        out_specs=pl.BlockSpec((1, MOD_ROWS, tn), lambda l, j: (l, 0, j)),
        out_shape=jax.ShapeDtypeStruct((depth, MOD_ROWS, n), F32),
        compiler_params=_params(2),
        name="adaln_mod",
    )(cvec, w_mod, b_mod.reshape(depth, 1, n))


def _mod_row(n_ctx, t_lat, tile=T_CTX):
    ctx_tiles = n_ctx * T_CTX // tile
    tiles_per_latent = t_lat // tile
    return lambda i: jnp.where(i < ctx_tiles, 0, 1 + (i - ctx_tiles) // tiles_per_latent)


def _mod_spec(row, which):
    return pl.BlockSpec((None, 1, D_MODEL), lambda i: (row(i), 0, which))


def _norm_mod_kernel(x_ref, g_ref, sh_ref, sc_ref, o_ref):
    y = _rms(x_ref[...], g_ref[0:1, :])
    o_ref[...] = (y * (1.0 + sc_ref[...]) + sh_ref[...]).astype(BF16)


def _norm_mod(x, g, mod, row):
    n_tok = x.shape[0]
    return pl.pallas_call(
        _norm_mod_kernel,
        grid=(n_tok // T_CTX,),
        in_specs=[pl.BlockSpec((T_CTX, D_MODEL), lambda i: (i, 0)),
                  pl.BlockSpec((4, D_MODEL), lambda i: (0, 0)),
                  _mod_spec(row, 0), _mod_spec(row, 1)],
        out_specs=pl.BlockSpec((T_CTX, D_MODEL), lambda i: (i, 0)),
        out_shape=jax.ShapeDtypeStruct((n_tok, D_MODEL), BF16),
        compiler_params=_params(1),
        name="norm_mod",
    )(x, g, mod, mod)


def _mm_kernel(cols_ref, a_ref, w_ref, o_ref, wb_ref):
    del cols_ref
    @pl.when(pl.program_id(1) == 0)
    def _():
        wb_ref[...] = w_ref[...].astype(BF16)

    o_ref[...] = jnp.dot(a_ref[...], wb_ref[...], preferred_element_type=F32).astype(o_ref.dtype)


def _matmul(a, w, col_blocks, tn, out_dtype, name):
    m, k = a.shape
    tm = 1024
    cols = jnp.asarray(col_blocks, jnp.int32)
    return pl.pallas_call(
        _mm_kernel,
        grid_spec=pltpu.PrefetchScalarGridSpec(
            num_scalar_prefetch=1,
            grid=(len(col_blocks), m // tm),
            in_specs=[pl.BlockSpec((tm, k), lambda j, i, c: (i, 0)),
                      pl.BlockSpec((k, tn), lambda j, i, c: (0, c[j]))],
            out_specs=pl.BlockSpec((tm, tn), lambda j, i, c: (i, j)),
            scratch_shapes=[pltpu.VMEM((k, tn), BF16)]),
        out_shape=jax.ShapeDtypeStruct((m, len(col_blocks) * tn), out_dtype),
        compiler_params=_params(2),
        name=name,
    )(cols, a, w)


U_RQ, U_RK, U_RV, U_RG, U_NQ, U_HQ, U_HFF, U_HFB, U_HI, U_HG = range(10)
U_MG = 5
U_MAIN_COLS = [0, 1, 2, 3, 4, 7, 8, 9, 10, 11, 12, 13, 14, 15, 16, 17]
U_KV_COLS = [5, 6]


def _rope(x, cos, sin):
    lane = lax.broadcasted_iota(jnp.int32, x.shape, 1)
    partner = jnp.where(lane % 64 < 32, pltpu.roll(x, 96, 1), pltpu.roll(x, 32, 1))
    return x * cos + partner * sin


def _ret_kernel(*refs, T, latent):
    if latent:
        q_ref, k_ref, v_ref, g_ref, lg_ref, cos_ref, sin_ref, s0_ref, o_ref = refs
    else:
        q_ref, k_ref, v_ref, g_ref, lg_ref, o_ref, sfin_ref = refs
    scale = HEAD_DIM ** -0.5
    tq = RET_QTILE
    pos = lax.broadcasted_iota(jnp.int32, (T, 1), 0).astype(F32)
    for h in range(N_HEADS):
        hs = slice(h * HEAD_DIM, (h + 1) * HEAD_DIM)
        lgf = lg_ref[0, h]
        lgb = lg_ref[1, h]
        qf = q_ref[:, hs].astype(F32)
        kf = k_ref[:, hs].astype(F32)
        if latent:
            qf = _rope(qf, cos_ref[...], sin_ref[...])
            kf = _rope(kf, cos_ref[...], sin_ref[...])
        qb = qf.astype(BF16)
        kb = kf.astype(BF16)
        vb = v_ref[:, hs]
        for qi in range(T // tq):
            rows = slice(qi * tq, (qi + 1) * tq)
            s = lax.dot_general(qb[rows], kb, NT_DIMS, preferred_element_type=F32) * scale
            diff = (lax.broadcasted_iota(jnp.int32, (tq, T), 0) + (qi * tq)
                    - lax.broadcasted_iota(jnp.int32, (tq, T), 1)).astype(F32)
            dec = jnp.exp(jnp.where(diff > 0, lgf, -lgb) * diff)
            dec = jnp.where(diff == 0, 2.0, dec)
            o = jnp.dot((s * dec).astype(BF16), vb, preferred_element_type=F32)
            if latent:
                n = pos[rows]
                qd_f = (qf[rows] * jnp.exp(lgf * (n + 1.0))).astype(BF16)
                qd_b = (qf[rows] * jnp.exp(lgb * (T - n))).astype(BF16)
                o = o + jnp.dot(qd_f, s0_ref[0, h].astype(BF16), preferred_element_type=F32)
                o = o + jnp.dot(qd_b, s0_ref[1, h].astype(BF16), preferred_element_type=F32)
            mu = jnp.mean(o, axis=-1, keepdims=True)
            oc = o - mu
            y = oc * lax.rsqrt(jnp.mean(oc * oc, axis=-1, keepdims=True) + LN_EPS)
            o_ref[rows, hs] = (y * _silu(g_ref[rows, hs].astype(F32))).astype(BF16)
        if not latent:
            kd_f = (kf * (scale * jnp.exp(lgf * (T - 1.0 - pos)))).T.astype(BF16)
            kd_b = (kf * (scale * jnp.exp(lgb * pos))).T.astype(BF16)
            sfin_ref[0, h] = jnp.dot(kd_f, vb, preferred_element_type=F32)
            sfin_ref[1, h] = jnp.dot(kd_b, vb, preferred_element_type=F32)


def _retention(u, lg, n_seq, T, row0, rope=None, s0=None, layer=None):
    latent = s0 is not None
    blk0 = row0 // T
    col = lambda c: pl.BlockSpec((T, BR_WIDTH), lambda b: (blk0 + b, c))
    in_specs = [col(U_RQ), col(U_RK), col(U_RV), col(U_RG), pl.BlockSpec(memory_space=pltpu.SMEM)]
    args = [u, u, u, u, lg]
    out_specs = [pl.BlockSpec((T, BR_WIDTH), lambda b: (b, 0))]
    out_shape = [jax.ShapeDtypeStruct((n_seq * T, BR_WIDTH), BF16)]
    if latent:
        in_specs += [pl.BlockSpec((T, HEAD_DIM), lambda b: (0, 0)), pl.BlockSpec((T, HEAD_DIM), lambda b: (0, 0)),
                     pl.BlockSpec((None, None, 2, N_HEADS, HEAD_DIM, HEAD_DIM), lambda b: (b, layer, 0, 0, 0, 0))]
        args += [rope[0], rope[1], s0]
    else:
        out_specs.append(pl.BlockSpec((None, 2, N_HEADS, HEAD_DIM, HEAD_DIM), lambda b: (b, 0, 0, 0, 0)))
        out_shape.append(jax.ShapeDtypeStruct((n_seq, 2, N_HEADS, HEAD_DIM, HEAD_DIM), F32))
    return pl.pallas_call(
        functools.partial(_ret_kernel, T=T, latent=latent),
        grid=(n_seq,), in_specs=in_specs, out_specs=out_specs, out_shape=out_shape,
        compiler_params=_params(1),
        name="retention_latent" if latent else "retention_ctx",
    )(*args)


def _ctx_attn_kernel(q_ref, kv_ref, o_ref):
    scale = HEAD_DIM ** -0.5
    for h in range(N_HEADS):
        hs = slice(h * HEAD_DIM, (h + 1) * HEAD_DIM)
        vs = slice(BR_WIDTH + h * HEAD_DIM, BR_WIDTH + (h + 1) * HEAD_DIM)
        s = lax.dot_general(q_ref[:, hs], kv_ref[:, hs].astype(BF16), NT_DIMS, preferred_element_type=F32) * scale
        p = jnp.exp(s - jnp.max(s, axis=-1, keepdims=True))
        o = jnp.dot(p.astype(BF16), kv_ref[:, vs].astype(BF16), preferred_element_type=F32)
        o_ref[:, hs] = (o / jnp.sum(p, axis=-1, keepdims=True)).astype(BF16)


def _ctx_attention(u, kv, n_seq):
    T = T_CTX
    return pl.pallas_call(
        _ctx_attn_kernel,
        grid=(n_seq,),
        in_specs=[pl.BlockSpec((T, BR_WIDTH), lambda b: (b, U_NQ)),
                  pl.BlockSpec((T, 2 * BR_WIDTH), lambda b: (b, 0))],
        out_specs=pl.BlockSpec((T, BR_WIDTH), lambda b: (b, 0)),
        out_shape=jax.ShapeDtypeStruct((n_seq * T, BR_WIDTH), BF16),
        compiler_params=_params(1),
        name="attention_ctx",
    )(u, kv)


def _na_kernel(q_ref, kv_ref, kc_ref, vc_ref, bias_ref, o_ref, kvb_ref, kcb_ref, vcb_ref):
    scale = HEAD_DIM ** -0.5
    rows = q_ref.shape[0] // GRID_W
    n_loc = NA_ROWS * GRID_W
    kvb_ref[...] = kv_ref[...].astype(BF16)
    kcb_ref[...] = kc_ref[...].astype(BF16)
    vcb_ref[...] = vc_ref[...].astype(BF16)
    for h in range(N_HEADS):
        hs = slice(h * HEAD_DIM, (h + 1) * HEAD_DIM)
        vs = slice(BR_WIDTH + h * HEAD_DIM, BR_WIDTH + (h + 1) * HEAD_DIM)

        def row_step(r, carry):
            rs = jnp.clip(r - NA_ROWS // 2, 0, rows - NA_ROWS)
            q0 = pl.multiple_of(r * GRID_W, GRID_W)
            k0 = pl.multiple_of(rs * GRID_W, GRID_W)
            q = q_ref[pl.ds(q0, GRID_W), hs]
            k_loc = kvb_ref[pl.ds(k0, n_loc), hs]
            v_loc = kvb_ref[pl.ds(k0, n_loc), vs]
            s_loc = lax.dot_general(q, k_loc, NT_DIMS, preferred_element_type=F32) * scale + bias_ref[h, r]
            s_ctx = lax.dot_general(q, kcb_ref[:, hs], NT_DIMS, preferred_element_type=F32) * scale
            m = jnp.maximum(jnp.max(s_loc, axis=-1, keepdims=True), jnp.max(s_ctx, axis=-1, keepdims=True))
            p_loc = jnp.exp(s_loc - m)
            p_ctx = jnp.exp(s_ctx - m)
            denom = jnp.sum(p_loc, axis=-1, keepdims=True) + jnp.sum(p_ctx, axis=-1, keepdims=True)
            o = (jnp.dot(p_loc.astype(BF16), v_loc, preferred_element_type=F32)
                 + jnp.dot(p_ctx.astype(BF16), vcb_ref[:, hs], preferred_element_type=F32))
            o_ref[pl.ds(q0, GRID_W), hs] = (o / denom).astype(BF16)
            return carry

        lax.fori_loop(0, rows, row_step, 0)


def _na_bias_table(rpb, rows):
    r = np.arange(rows)
    rs = np.clip(r - NA_ROWS // 2, 0, rows - NA_ROWS)
    row_off = rs[:, None] + np.arange(NA_ROWS)[None, :] - r[:, None] + (NA_ROWS - 1)
    qc = np.arange(GRID_W)
    cs = np.clip(qc - NA_COLS // 2, 0, GRID_W - NA_COLS)
    kc = np.arange(GRID_W)
    col_ok = (kc[None, :] >= cs[:, None]) & (kc[None, :] < cs[:, None] + NA_COLS)
    col_off = np.clip(kc[None, :] - qc[:, None] + NA_COLS - 1, 0, 2 * NA_COLS - 2)
    b = rpb[:, row_off][:, :, :, col_off]
    b = jnp.where(col_ok[None, None, None], b.astype(F32), NEG_INF)
    return b.transpose(0, 1, 3, 2, 4).reshape(N_HEADS, rows, GRID_W, NA_ROWS * GRID_W)


def _na_attention(u, kv, cache_k, cache_v, bias, n_seq, T, row0, layer):
    blk0 = row0 // T
    past = cache_k.shape[2]
    rows = T // GRID_W
    cache_spec = pl.BlockSpec((None, None, past, BR_WIDTH), lambda b: (b, layer, 0, 0))
    return pl.pallas_call(
        _na_kernel,
        grid=(n_seq,),
        in_specs=[pl.BlockSpec((T, BR_WIDTH), lambda b: (blk0 + b, U_NQ)),
                  pl.BlockSpec((T, 2 * BR_WIDTH), lambda b: (blk0 + b, 0)),
                  cache_spec, cache_spec,
                  pl.BlockSpec((N_HEADS, rows, GRID_W, NA_ROWS * GRID_W), lambda b: (0, 0, 0, 0))],
        out_specs=pl.BlockSpec((T, BR_WIDTH), lambda b: (b, 0)),
        out_shape=jax.ShapeDtypeStruct((n_seq * T, BR_WIDTH), BF16),
        scratch_shapes=[pltpu.VMEM((T, 2 * BR_WIDTH), BF16), pltpu.VMEM((past, BR_WIDTH), BF16),
                        pltpu.VMEM((past, BR_WIDTH), BF16)],
        compiler_params=_params(1),
        name="attention_latent",
    )(u, kv, cache_k, cache_v, bias)


def _hgrn_kernel(*refs, T, latent):
    if latent:
        (hq_ref, hff_ref, hfb_ref, hi_ref, hg_ref, lb_ref, gn_ref, s0_ref, o_ref,
         lf_ref, kk_ref, qs_ref, od_ref, st_ref) = refs
    else:
        (hq_ref, hff_ref, hfb_ref, hi_ref, hg_ref, lb_ref, gn_ref, o_ref, sfin_ref,
         lf_ref, kk_ref, qs_ref, od_ref, st_ref) = refs
    cb = HG_BLOCK
    for d, f_ref in enumerate((hff_ref, hfb_ref)):
        lb = lb_ref[d:d + 1, :]
        f = jnp.maximum(lb + (1.0 - lb) * jax.nn.sigmoid(f_ref[...].astype(F32)), F_MIN)
        lf_ref[d] = jnp.log(f)
        kk_ref[d] = 1.0 - f
    qs_ref[...] = _silu(hq_ref[...].astype(F32))
    for d in range(2):
        for h in range(N_HEADS):
            st_ref[d, h] = s0_ref[d, h].T if latent else jnp.zeros((HEAD_DIM, HEAD_DIM), F32)

    ri = lax.broadcasted_iota(jnp.int32, (cb, cb), 0)
    ci = lax.broadcasted_iota(jnp.int32, (cb, cb), 1)
    tri = ((ci <= ri).astype(F32), (ci >= ri).astype(F32))
    row = lax.broadcasted_iota(jnp.int32, (cb, 1), 0)

    def block_step(j, carry):
        for d in range(2):
            base = pl.multiple_of(j * cb if d == 0 else T - (j + 1) * cb, cb)
            rows = pl.ds(base, cb)
            a_all = jnp.dot(tri[d], lf_ref[d, rows, :], preferred_element_type=F32, precision=HIGHEST)
            k_all = kk_ref[d, rows, :]
            q_all = qs_ref[rows, :]
            v_all = hi_ref[rows, :].astype(F32)
            last = cb - 1 if d == 0 else 0
            for h in range(N_HEADS):
                hs = slice(h * HEAD_DIM, (h + 1) * HEAD_DIM)
                a, k, q, v = a_all[:, hs], k_all[:, hs], q_all[:, hs], v_all[:, hs]
                st = st_ref[d, h]
                o = lax.dot_general((q * jnp.exp(a)).astype(BF16), st.astype(BF16), NT_DIMS,
                                    preferred_element_type=F32)
                for n in range(cb):
                    e = k * jnp.exp(jnp.minimum(a[n:n + 1, :] - a, 0.0)) * q[n:n + 1, :]
                    att = jnp.sum(e, axis=-1, keepdims=True)
                    att = jnp.where((row <= n) if d == 0 else (row >= n), att, 0.0)
                    o_n = jnp.sum(att * v, axis=0, keepdims=True)
                    o = o + jnp.where(row == n, o_n, 0.0)
                od_ref[d, rows, hs] = o
                a_tot = a[last:last + 1, :]
                k_hat = (k * jnp.exp(a_tot - a)).astype(BF16)
                upd = lax.dot_general(v.astype(BF16), k_hat, TN_DIMS, preferred_element_type=F32)
                st_ref[d, h] = st * jnp.exp(a_tot) + upd
        return carry

    lax.fori_loop(0, T // cb, block_step, 0)

    for h in range(N_HEADS):
        hs = slice(h * HEAD_DIM, (h + 1) * HEAD_DIM)
        o = od_ref[0, :, hs] + od_ref[1, :, hs]
        y = _rms(o, gn_ref[:, hs])
        o_ref[:, hs] = (y * _silu(hg_ref[:, hs].astype(F32))).astype(BF16)
    if not latent:
        for d in range(2):
            for h in range(N_HEADS):
                sfin_ref[d, h] = st_ref[d, h].T


def _hgrn(u, lb, gnorm, n_seq, T, row0, s0=None, layer=None):
    latent = s0 is not None
    blk0 = row0 // T
    col = lambda c: pl.BlockSpec((T, BR_WIDTH), lambda b: (blk0 + b, c))
    in_specs = [col(U_HQ), col(U_HFF), col(U_HFB), col(U_HI), col(U_HG),
                pl.BlockSpec((2, BR_WIDTH), lambda b: (0, 0)), pl.BlockSpec((1, BR_WIDTH), lambda b: (0, 0))]
    args = [u, u, u, u, u, lb, gnorm]
    out_specs = [pl.BlockSpec((T, BR_WIDTH), lambda b: (b, 0))]
    out_shape = [jax.ShapeDtypeStruct((n_seq * T, BR_WIDTH), BF16)]
    state_block = (None, 2, N_HEADS, HEAD_DIM, HEAD_DIM)
    if latent:
        in_specs.append(pl.BlockSpec((None,) + state_block, lambda b: (b, layer, 0, 0, 0, 0)))
        args.append(s0)
    else:
        out_specs.append(pl.BlockSpec(state_block, lambda b: (b, 0, 0, 0, 0)))
        out_shape.append(jax.ShapeDtypeStruct((n_seq, 2, N_HEADS, HEAD_DIM, HEAD_DIM), F32))
    return pl.pallas_call(
        functools.partial(_hgrn_kernel, T=T, latent=latent),
        grid=(n_seq,), in_specs=in_specs, out_specs=out_specs, out_shape=out_shape,
        scratch_shapes=[pltpu.VMEM((2, T, BR_WIDTH), F32), pltpu.VMEM((2, T, BR_WIDTH), F32),
                        pltpu.VMEM((T, BR_WIDTH), F32), pltpu.VMEM((2, T, BR_WIDTH), F32),
                        pltpu.VMEM((2, N_HEADS, HEAD_DIM, HEAD_DIM), F32)],
        compiler_params=_params(1),
        name="hgrn_latent" if latent else "hgrn_ctx",
    )(*args)


def _merge_kernel(ret_ref, na_ref, hg_ref, mg0_ref, mg1_ref, mg2_ref, x_ref, wbr_ref, wo_ref, g_ref,
                  gate_ref, sh_ref, sc_ref, xo_ref, h_ref, wbr_b, wo_b):
    @pl.when(pl.program_id(0) == 0)
    def _():
        wbr_b[...] = wbr_ref[...].astype(BF16)
        wo_b[...] = wo_ref[...].astype(BF16)

    acc = None
    for n, (br, mg) in enumerate(((ret_ref, mg0_ref), (na_ref, mg1_ref), (hg_ref, mg2_ref))):
        proj = jnp.dot(br[...], wbr_b[n], preferred_element_type=F32)
        term = jax.nn.sigmoid(mg[...].astype(F32)) * proj
        acc = term if acc is None else acc + term
    m = jnp.dot(acc.astype(BF16), wo_b[...], preferred_element_type=F32)
    x = x_ref[...] + gate_ref[...] * _rms(m, g_ref[1:2, :])
    xo_ref[...] = x
    h_ref[...] = (_rms(x, g_ref[2:3, :]) * (1.0 + sc_ref[...]) + sh_ref[...]).astype(BF16)


def _merge(ret_o, na_o, hg_o, u, x, w_br, w_o, g, mod, row):
    n_tok = x.shape[0]
    tm = T_CTX
    br = pl.BlockSpec((tm, BR_WIDTH), lambda i: (i, 0))
    mg = lambda n: pl.BlockSpec((tm, D_MODEL), lambda i: (i, U_MG + n))
    xs = pl.BlockSpec((tm, D_MODEL), lambda i: (i, 0))
    return pl.pallas_call(
        _merge_kernel,
        grid=(n_tok // tm,),
        in_specs=[br, br, br, mg(0), mg(1), mg(2), xs,
                  pl.BlockSpec((3, BR_WIDTH, D_MODEL), lambda i: (0, 0, 0)),
                  pl.BlockSpec((D_MODEL, D_MODEL), lambda i: (0, 0)),
                  pl.BlockSpec((4, D_MODEL), lambda i: (0, 0)),
                  _mod_spec(row, 2), _mod_spec(row, 3), _mod_spec(row, 4)],
        out_specs=[xs, xs],
        out_shape=[jax.ShapeDtypeStruct((n_tok, D_MODEL), F32), jax.ShapeDtypeStruct((n_tok, D_MODEL), BF16)],
        scratch_shapes=[pltpu.VMEM((3, BR_WIDTH, D_MODEL), BF16), pltpu.VMEM((D_MODEL, D_MODEL), BF16)],
        compiler_params=_params(1),
        name="merge",
    )(ret_o, na_o, hg_o, u, u, u, x, w_br, w_o, g, mod, mod, mod)


FF_TN = D_FF // 2


def _router_kernel(h_ref, w_ref, o_ref):
    logits = jnp.dot(h_ref[...].astype(F32), w_ref[...], preferred_element_type=F32, precision=HIGHEST)
    lane = lax.broadcasted_iota(jnp.int32, logits.shape, 1)
    lowest = jnp.finfo(F32).min
    lg = jnp.where(lane < N_EXPERTS, logits, lowest)
    v1 = jnp.max(lg, axis=-1, keepdims=True)
    i1 = jnp.min(jnp.where(lg == v1, lane, LANES), axis=-1, keepdims=True)
    lg2 = jnp.where(lane == i1, lowest, lg)
    v2 = jnp.max(lg2, axis=-1, keepdims=True)
    i2 = jnp.min(jnp.where(lg2 == v2, lane, LANES), axis=-1, keepdims=True)
    e2 = jnp.exp(v2 - v1)
    denom = 1.0 + e2
    o_ref[...] = jnp.where(lane == i1, 1.0 / denom, 0.0) + jnp.where(lane == i2, e2 / denom, 0.0)


def _router(h, w_router):
    n_tok = h.shape[0]
    tm = 1024
    w = jnp.pad(w_router, ((0, 0), (0, LANES - N_EXPERTS)))
    return pl.pallas_call(
        _router_kernel,
        grid=(n_tok // tm,),
        in_specs=[pl.BlockSpec((tm, D_MODEL), lambda i: (i, 0)), pl.BlockSpec((D_MODEL, LANES), lambda i: (0, 0))],
        out_specs=pl.BlockSpec((tm, LANES), lambda i: (i, 0)),
        out_shape=jax.ShapeDtypeStruct((n_tok, LANES), F32),
        compiler_params=_params(1),
        name="router",
    )(h, w)


def _gate_up_kernel(*refs, gated, tiles_per_expert):
    if gated:
        a_ref, wg_ref, wu_ref, gate_ref, o_ref, wg_b, wu_b = refs
    else:
        a_ref, wg_ref, wu_ref, o_ref, wg_b, wu_b = refs

    @pl.when(pl.program_id(1) == 0)
    def _():
        wg_b[...] = wg_ref[...].astype(BF16)
        wu_b[...] = wu_ref[...].astype(BF16)

    a = a_ref[...]
    act = _silu(jnp.dot(a, wg_b[...], preferred_element_type=F32)) * jnp.dot(a, wu_b[...], preferred_element_type=F32)
    if gated:
        e = pl.program_id(0) // tiles_per_expert
        lane = lax.broadcasted_iota(jnp.int32, gate_ref.shape, 1)
        act = act * jnp.sum(jnp.where(lane == e, gate_ref[...], 0.0), axis=-1, keepdims=True)
    o_ref[...] = act.astype(BF16)


def _gate_up(h, w_gu, gate=None):
    n_tok = h.shape[0]
    n_exp = w_gu.shape[0]
    tm, tn = 1024, FF_TN
    tpe = D_FF // tn
    gated = gate is not None
    in_specs = [pl.BlockSpec((tm, D_MODEL), lambda j, i: (i, 0)),
                pl.BlockSpec((None, D_MODEL, tn), lambda j, i: (j // tpe, 0, j % tpe)),
                pl.BlockSpec((None, D_MODEL, tn), lambda j, i: (j // tpe, 0, j % tpe + tpe))]
    args = [h, w_gu, w_gu]
    if gated:
        in_specs.append(pl.BlockSpec((tm, LANES), lambda j, i: (i, 0)))
        args.append(gate)
    return pl.pallas_call(
        functools.partial(_gate_up_kernel, gated=gated, tiles_per_expert=tpe),
        grid=(n_exp * tpe, n_tok // tm),
        in_specs=in_specs,
        out_specs=pl.BlockSpec((tm, tn), lambda j, i: (i, j)),
        out_shape=jax.ShapeDtypeStruct((n_tok, n_exp * D_FF), BF16),
        scratch_shapes=[pltpu.VMEM((D_MODEL, tn), BF16), pltpu.VMEM((D_MODEL, tn), BF16)],
        compiler_params=_params(2),
        name="ffn_gate_up",
    )(*args)


def _down_kernel(a_ref, w_ref, x_ref, g_ref, gate_ref, o_ref, acc_ref):
    kk = pl.program_id(1)

    @pl.when(kk == 0)
    def _():
        acc_ref[...] = jnp.zeros_like(acc_ref)

    acc_ref[...] += jnp.dot(a_ref[...], w_ref[...].astype(BF16), preferred_element_type=F32)

    @pl.when(kk == pl.num_programs(1) - 1)
    def _():
        o_ref[...] = x_ref[...] + gate_ref[...] * _rms(acc_ref[...], g_ref[3:4, :])


DOWN_TM = 1024


def _down(act, w_down, x, g, mod, row):
    n_tok, k_tot = act.shape
    tm, tk = DOWN_TM, FF_TN
    xs = pl.BlockSpec((tm, D_MODEL), lambda i, k: (i, 0))
    return pl.pallas_call(
        _down_kernel,
        grid=(n_tok // tm, k_tot // tk),
        in_specs=[pl.BlockSpec((tm, tk), lambda i, k: (i, k)),
                  pl.BlockSpec((tk, D_MODEL), lambda i, k: (k, 0)),
                  xs, pl.BlockSpec((4, D_MODEL), lambda i, k: (0, 0)),
                  pl.BlockSpec((None, 1, D_MODEL), lambda i, k: (row(i), 0, 5))],
        out_specs=xs,
        out_shape=jax.ShapeDtypeStruct((n_tok, D_MODEL), F32),
        scratch_shapes=[pltpu.VMEM((tm, D_MODEL), F32)],
        compiler_params=_params(2),
        name="ffn_down",
    )(act, w_down, x, g, mod)


def _rope_tables(T):
    nf = HEAD_DIM // 4
    t = np.arange(T)
    pos = np.stack([t // GRID_W, t % GRID_W], axis=-1).astype(np.float64)
    inv = ROPE_BASE ** (-np.arange(nf, dtype=np.float64) / nf)
    d = np.arange(HEAD_DIM)
    ang = pos[:, d // 64] * inv[d % nf][None, :]
    sign = np.where(d % 64 < nf, -1.0, 1.0)
    return jnp.asarray(np.cos(ang), F32), jnp.asarray(np.sin(ang) * sign[None, :], F32)


def kernel(x_prompt, x_sample, c, cache_na_k, cache_na_v, state_ret, state_hgrn, c_ctx, w_mod, b_mod, norm_g,
           w_in, ret_decay, na_rpb, hg_lb, hg_norm, w_br, w_o, w_ffn_gu, w_ffn_down, w_router, w_exp_gu, w_exp_down):
    n_ctx, t_ctx, d_model = x_prompt.shape
    n_lat, t_lat, _ = x_sample.shape
    depth = w_in.shape[0]
    past = cache_na_k.shape[2]
    assert t_ctx == T_CTX and d_model == D_MODEL and t_lat % T_CTX == 0 and 1 + n_lat <= MOD_ROWS
    n_ctx_tok = n_ctx * t_ctx

    x = jnp.concatenate([x_prompt.reshape(n_ctx_tok, D_MODEL), x_sample.reshape(n_lat * t_lat, D_MODEL)], axis=0)
    cvec = jnp.concatenate([c_ctx[None], c, jnp.zeros((MOD_ROWS - 1 - n_lat, D_MODEL), F32)], axis=0)
    mods = _modulation(cvec, w_mod, b_mod).reshape(depth, MOD_ROWS, 1, N_MOD * D_MODEL)

    p_lb = jax.nn.softmax(hg_lb.astype(F32), axis=0)
    hg_lower = jnp.cumsum(p_lb, axis=0) - p_lb[:1]
    log_gamma = jax.nn.log_sigmoid(ret_decay.astype(F32))
    rope = _rope_tables(t_lat)
    cache_k = cache_na_k.reshape(n_lat, depth, past, BR_WIDTH)
    cache_v = cache_na_v.reshape(n_lat, depth, past, BR_WIDTH)

    new_k, new_v, new_sr, new_sh = [], [], [], []
    assert n_ctx_tok % DOWN_TM == 0 and t_lat % DOWN_TM == 0
    row = _mod_row(n_ctx, t_lat)
    row_down = _mod_row(n_ctx, t_lat, DOWN_TM)
    for l in range(depth):
        g, mod = norm_g[l], mods[l]
        h = _norm_mod(x, g, mod, row)
        u = _matmul(h, w_in[l], U_MAIN_COLS, BR_WIDTH, BF16, "in_proj")
        kv = _matmul(h, w_in[l], U_KV_COLS, BR_WIDTH, F32, "in_proj_kv")

        ret_c, sr = _retention(u, log_gamma[l], n_ctx, t_ctx, 0)
        ret_l = _retention(u, log_gamma[l], n_lat, t_lat, n_ctx_tok, rope=rope, s0=state_ret, layer=l)[0]
        na_c = _ctx_attention(u, kv, n_ctx)
        bias = _na_bias_table(na_rpb[l], t_lat // GRID_W)
        na_l = _na_attention(u, kv, cache_k, cache_v, bias, n_lat, t_lat, n_ctx_tok, l)
        gnorm = hg_norm[l].reshape(1, BR_WIDTH)
        hg_c, sh = _hgrn(u, hg_lower[l], gnorm, n_ctx, t_ctx, 0)
        hg_l = _hgrn(u, hg_lower[l], gnorm, n_lat, t_lat, n_ctx_tok, s0=state_hgrn, layer=l)[0]

        ret_o = jnp.concatenate([ret_c, ret_l], axis=0)
        na_o = jnp.concatenate([na_c, na_l], axis=0)
        hg_o = jnp.concatenate([hg_c, hg_l], axis=0)
        x, h2 = _merge(ret_o, na_o, hg_o, u, x, w_br[l], w_o[l], g, mod, row)

        if l % 2 == 0:
            act = _gate_up(h2, w_ffn_gu[l // 2][None])
            x = _down(act, w_ffn_down[l // 2], x, g, mod, row_down)
        else:
            gate = _router(h2, w_router[l // 2])
            act = _gate_up(h2, w_exp_gu[l // 2], gate)
            x = _down(act, w_exp_down[l // 2].reshape(N_EXPERTS * D_FF, D_MODEL), x, g, mod, row_down)

        new_k.append(kv[:n_ctx_tok, :BR_WIDTH].reshape(n_ctx, t_ctx, N_HEADS, HEAD_DIM))
        new_v.append(kv[:n_ctx_tok, BR_WIDTH:].reshape(n_ctx, t_ctx, N_HEADS, HEAD_DIM))
        new_sr.append(sr)
        new_sh.append(sh)

    y_prompt = x[:n_ctx_tok].reshape(n_ctx, t_ctx, D_MODEL)
    y_sample = x[n_ctx_tok:].reshape(n_lat, t_lat, D_MODEL)
    return (y_prompt, y_sample, jnp.stack(new_k, axis=1), jnp.stack(new_v, axis=1),
            jnp.stack(new_sr, axis=1), jnp.stack(new_sh, axis=1))
```
